```python
import math
import jax, jax.numpy as jnp
from jax import lax
import numpy as np

D_MODEL = 1024
BATCH = 2
SEQ = 8192
DEPTH = 1
DEC_BATCH = 16
DEC_SEQ = 64
PAST_LEN = 4096

CHUNK = 64
EPS = 1e-6
GLA_HEADS = 4
GLA_DK = 64
GLA_DV = 128
GLA_RANK = 16
GLA_TAU = 16.0
FOX_HEADS = 8
FOX_DH = 64
MEM_HEADS = 4
MEM_DH = 128
N_MEM = 256
N_BRANCH = 3
BRANCH_W = 512
Q_BLOCK = 128
PEER_HEADS = 8
N_KEYS = 128
N_EXPERTS = N_KEYS * N_KEYS
PEER_TOPK = 16
PEER_DKEY = 256
PEER_DHALF = PEER_DKEY // 2
PEER_TOK_BLOCK = 128

IN_SIZES = (GLA_HEADS * GLA_DK, GLA_HEADS * GLA_DK, GLA_HEADS * GLA_DV, GLA_HEADS * GLA_DV, GLA_RANK,
            FOX_HEADS * FOX_DH, FOX_HEADS * FOX_DH, FOX_HEADS * FOX_DH, FOX_HEADS,
            MEM_HEADS * MEM_DH, N_BRANCH * D_MODEL)
IN_TOTAL = sum(IN_SIZES)

kernel_name = "hybrid_gla_fox_mem_peer_stream_step"


def rmsnorm(x, g):
    xf = x.astype(jnp.float32)
    y = xf * lax.rsqrt(jnp.mean(xf * xf, axis=-1, keepdims=True) + EPS)
    return (y * g.astype(jnp.float32)).astype(x.dtype)


def split_cols(p):
    idx = np.cumsum(IN_SIZES)[:-1].tolist()
    return jnp.split(p, idx, axis=-1)


def gla_recurrence(q, k, v, log_a, s0):
    B, T, H, DK = q.shape
    DV = v.shape[-1]
    L = min(CHUNK, T)
    n = T // L

    def blocks(a):
        return a.reshape(B, n, L, H, a.shape[-1]).transpose(1, 0, 3, 2, 4)

    qc = blocks(q.astype(jnp.float32)) * (DK ** -0.5)
    kc = blocks(k.astype(jnp.float32))
    vc = blocks(v.astype(jnp.float32))
    b = jnp.cumsum(blocks(log_a.astype(jnp.float32)), axis=3)
    b_last = b[:, :, :, -1:, :]
    q_dec = qc * jnp.exp(b)
    k_dec = kc * jnp.exp(-b)
    k_out = kc * jnp.exp(b_last - b)
    causal = jnp.tril(jnp.ones((L, L), dtype=bool))
    att = jnp.where(causal, jnp.einsum('nbhld,nbhmd->nbhlm', q_dec, k_dec), 0.0)
    o_intra = jnp.einsum('nbhlm,nbhme->nbhle', att, vc)

    def step(S, inp):
        qd, ko, vv, dl = inp
        o = jnp.einsum('bhld,bhde->bhle', qd, S)
        S = S * dl[..., 0, :, None] + jnp.einsum('bhld,bhle->bhde', ko, vv)
        return S, o

    s_fin, o_inter = lax.scan(step, s0.astype(jnp.float32), (q_dec, k_out, vc, jnp.exp(b_last)))
    o = (o_intra + o_inter).transpose(1, 0, 3, 2, 4).reshape(B, T, H, DV)
    return o.astype(q.dtype), s_fin


def fox_attention(q, k, v, c_q, c_k):
    B, Tq, H, d = q.shape
    Tk = k.shape[1]
    blk = min(Q_BLOCK, Tq)
    nb = Tq // blk
    q_pos = (Tk - Tq) + jnp.arange(Tq)
    k_pos = jnp.arange(Tk)
    c_kT = c_k.astype(jnp.float32).transpose(0, 2, 1)[:, :, None, :]
    qb = q.reshape(B, nb, blk, H, d).transpose(1, 0, 2, 3, 4)
    cqb = c_q.astype(jnp.float32).reshape(B, nb, blk, H).transpose(1, 0, 3, 2)
    pb = q_pos.reshape(nb, blk)
    scale = d ** -0.5

    def one_block(args):
        qq, cq, pp = args
        s = jnp.einsum('bqhd,bkhd->bhqk', qq, k, preferred_element_type=jnp.float32) * scale
        s = s + cq[..., None] - c_kT
        s = jnp.where(k_pos[None, :] <= pp[:, None], s, -jnp.inf)
        p = jax.nn.softmax(s, axis=-1)
        return jnp.einsum('bhqk,bkhd->bqhd', p.astype(v.dtype), v)

    o = lax.map(one_block, (qb, cqb, pb))
    return o.transpose(1, 0, 2, 3, 4).reshape(B, Tq, H, d)


def mem_attention(q, k, v):
    s = jnp.einsum('bqhd,bkhd->bhqk', q, k, preferred_element_type=jnp.float32) * (MEM_DH ** -0.5)
    p = jax.nn.softmax(s, axis=-1)
    return jnp.einsum('bhqk,bkhd->bqhd', p.astype(v.dtype), v)


def mem_keys_values(mem, g_in, w_kv, g_k):
    B = mem.shape[0]
    kv = rmsnorm(mem, g_in) @ w_kv
    k, v = jnp.split(kv, 2, axis=-1)
    k = rmsnorm(k.reshape(B, N_MEM, MEM_HEADS, MEM_DH), g_k)
    v = v.reshape(B, N_MEM, MEM_HEADS, MEM_DH)
    return k, v


def peer_ffn(h, w_q, sub_keys, u_tab, v_tab):
    B, T, D = h.shape
    n = B * T
    blk = min(PEER_TOK_BLOCK, n)
    pad = (-n) % blk
    hb = jnp.pad(h.reshape(n, D), ((0, pad), (0, 0))).reshape(-1, blk, D)

    def one_block(xb):
        q = (xb @ w_q).reshape(blk, PEER_HEADS, 2, PEER_DHALF)
        s = jnp.einsum('thpc,hpkc->thpk', q, sub_keys, preferred_element_type=jnp.float32)
        s1, i1 = lax.top_k(s[:, :, 0], PEER_TOPK)
        s2, i2 = lax.top_k(s[:, :, 1], PEER_TOPK)
        cand = (s1[..., :, None] + s2[..., None, :]).reshape(blk, PEER_HEADS, PEER_TOPK * PEER_TOPK)
        cidx = (i1[..., :, None] * N_KEYS + i2[..., None, :]).reshape(blk, PEER_HEADS, PEER_TOPK * PEER_TOPK)
        top, pos = lax.top_k(cand, PEER_TOPK)
        idx = jnp.take_along_axis(cidx, pos, axis=-1)
        g = jax.nn.softmax(top, axis=-1)
        u = jnp.take(u_tab, idx, axis=0)
        a = jax.nn.gelu(jnp.einsum('thkd,td->thk', u, xb, preferred_element_type=jnp.float32), approximate=False)
        w = (g * a).astype(xb.dtype)
        return jnp.einsum('thk,thkd->td', w, jnp.take(v_tab, idx, axis=0))

    out = lax.map(one_block, hb)
    return out.reshape(-1, D)[:n].reshape(B, T, D)


def trunk_layer(x, mem_k, mem_v, fox_past, gla_s0, g_attn, w_in, gla_w_a2, gla_b_a, gla_g_out, fox_b_f,
                fox_g_q, fox_g_k, mem_g_q, w_branch, w_out, g_ffn, peer_w_q, peer_sub_keys, peer_u, peer_v):
    B, T, _ = x.shape
    h = rmsnorm(x, g_attn)
    gq, gk, gv, gg, ga, fq, fk, fv, ff, mq, gates = split_cols(h @ w_in)
    log_a = jax.nn.log_sigmoid((ga @ gla_w_a2 + gla_b_a).astype(jnp.float32)) / GLA_TAU
    o_gla, s_new = gla_recurrence(gq.reshape(B, T, GLA_HEADS, GLA_DK), gk.reshape(B, T, GLA_HEADS, GLA_DK),
                                  gv.reshape(B, T, GLA_HEADS, GLA_DV), log_a.reshape(B, T, GLA_HEADS, GLA_DK), gla_s0)
    o_gla = (rmsnorm(o_gla, gla_g_out) * jax.nn.silu(gg.reshape(B, T, GLA_HEADS, GLA_DV))).reshape(B, T, BRANCH_W)
    fq = rmsnorm(fq.reshape(B, T, FOX_HEADS, FOX_DH), fox_g_q)
    fk = rmsnorm(fk.reshape(B, T, FOX_HEADS, FOX_DH), fox_g_k)
    fv = fv.reshape(B, T, FOX_HEADS, FOX_DH)
    logf = jax.nn.log_sigmoid((ff + fox_b_f).astype(jnp.float32))
    if fox_past is None:
        k_all, v_all, logf_all = fk, fv, logf
    else:
        k_all = jnp.concatenate([fox_past[0], fk], axis=1)
        v_all = jnp.concatenate([fox_past[1], fv], axis=1)
        logf_all = jnp.concatenate([fox_past[2].astype(jnp.float32), logf], axis=1)
    c_all = jnp.cumsum(logf_all, axis=1)
    o_fox = fox_attention(fq, k_all, v_all, c_all[:, -T:], c_all).reshape(B, T, BRANCH_W)
    mq = rmsnorm(mq.reshape(B, T, MEM_HEADS, MEM_DH), mem_g_q)
    o_mem = mem_attention(mq, mem_k, mem_v).reshape(B, T, BRANCH_W)
    proj = jnp.einsum('nbtc,ncd->nbtd', jnp.stack([o_gla, o_fox, o_mem], axis=0), w_branch)
    gate = jax.nn.sigmoid(gates.reshape(B, T, N_BRANCH, D_MODEL))
    x = x + jnp.einsum('btnd,nbtd->btd', gate, proj) @ w_out
    x = x + peer_ffn(rmsnorm(x, g_ffn), peer_w_q, peer_sub_keys, peer_u, peer_v)
    return x, fk, fv, logf, s_new


def setup_inputs(seed: int = 0) -> dict:
    key = jax.random.key(seed)
    ks = list(jax.random.split(key, 40))

    def nrm(i, shape, scale):
        return jax.random.normal(ks[i], shape, jnp.float32) * scale

    D = D_MODEL
    return {
        "x_prompt": nrm(0, (BATCH, SEQ, D), 1.0),
        "x_sample": nrm(1, (DEC_BATCH, DEC_SEQ, D), 1.0),
        "mem_prompt": nrm(2, (BATCH, N_MEM, D), 1.0),
        "cache_fox_k": nrm(3, (DEPTH, DEC_BATCH, PAST_LEN, FOX_HEADS, FOX_DH), 1.0),
        "cache_fox_v": nrm(4, (DEPTH, DEC_BATCH, PAST_LEN, FOX_HEADS, FOX_DH), 1.0),
        "cache_fox_logf": jax.nn.log_sigmoid(2.0 + nrm(5, (DEPTH, DEC_BATCH, PAST_LEN, FOX_HEADS), 1.0)),
        "state_gla": nrm(6, (DEPTH, DEC_BATCH, GLA_HEADS, GLA_DK, GLA_DV), 0.5),
        "cache_mem_k": nrm(7, (DEPTH, DEC_BATCH, N_MEM, MEM_HEADS, MEM_DH), 1.0),
        "cache_mem_v": nrm(8, (DEPTH, DEC_BATCH, N_MEM, MEM_HEADS, MEM_DH), 1.0),
        "g_attn": 1.0 + nrm(9, (DEPTH, D), 0.02),
        "w_in": nrm(10, (DEPTH, D, IN_TOTAL), D ** -0.5),
        "gla_w_a2": nrm(11, (DEPTH, GLA_RANK, GLA_HEADS * GLA_DK), GLA_RANK ** -0.5),
        "gla_b_a": nrm(12, (DEPTH, GLA_HEADS * GLA_DK), 0.1),
        "gla_g_out": 1.0 + nrm(13, (DEPTH, GLA_DV), 0.02),
        "fox_b_f": 2.0 + nrm(14, (DEPTH, FOX_HEADS), 0.1),
        "fox_g_q": 1.0 + nrm(15, (DEPTH, FOX_DH), 0.02),
        "fox_g_k": 1.0 + nrm(16, (DEPTH, FOX_DH), 0.02),
        "mem_g_q": 1.0 + nrm(17, (DEPTH, MEM_DH), 0.02),
        "mem_g_in": 1.0 + nrm(18, (DEPTH, D), 0.02),
        "mem_w_kv": nrm(19, (DEPTH, D, 2 * MEM_HEADS * MEM_DH), D ** -0.5),
        "mem_g_k": 1.0 + nrm(20, (DEPTH, MEM_DH), 0.02),
        "w_branch": nrm(21, (DEPTH, N_BRANCH, BRANCH_W, D), BRANCH_W ** -0.5),
        "w_out": nrm(22, (DEPTH, D, D), D ** -0.5),
        "g_ffn": 1.0 + nrm(23, (DEPTH, D), 0.02),
        "peer_w_q": nrm(24, (DEPTH, D, PEER_HEADS * PEER_DKEY), D ** -0.5),
        "peer_sub_keys": nrm(25, (DEPTH, PEER_HEADS, 2, N_KEYS, PEER_DHALF), PEER_DHALF ** -0.5),
        "peer_u": nrm(26, (DEPTH, N_EXPERTS, D), D ** -0.5),
        "peer_v": nrm(27, (DEPTH, N_EXPERTS, D), PEER_HEADS ** -0.5),
    }


def reference(x_prompt, x_sample, mem_prompt, cache_fox_k, cache_fox_v, cache_fox_logf, state_gla,
              cache_mem_k, cache_mem_v, g_attn, w_in, gla_w_a2, gla_b_a, gla_g_out, fox_b_f, fox_g_q, fox_g_k,
              mem_g_q, mem_g_in, mem_w_kv, mem_g_k, w_branch, w_out, g_ffn, peer_w_q, peer_sub_keys, peer_u, peer_v):
    yp, ys = x_prompt, x_sample
    fkp, fvp, flp, sgp, mkp, mvp = [], [], [], [], [], []
    fks, fvs, fls, sgs = [], [], [], []
    for l in range(DEPTH):
        lw = dict(g_attn=g_attn[l], w_in=w_in[l], gla_w_a2=gla_w_a2[l], gla_b_a=gla_b_a[l], gla_g_out=gla_g_out[l],
                  fox_b_f=fox_b_f[l], fox_g_q=fox_g_q[l], fox_g_k=fox_g_k[l], mem_g_q=mem_g_q[l],
                  w_branch=w_branch[l], w_out=w_out[l], g_ffn=g_ffn[l], peer_w_q=peer_w_q[l],
                  peer_sub_keys=peer_sub_keys[l], peer_u=peer_u[l], peer_v=peer_v[l])
        mk, mv = mem_keys_values(mem_prompt, mem_g_in[l], mem_w_kv[l], mem_g_k[l])
        s0 = jnp.zeros((yp.shape[0], GLA_HEADS, GLA_DK, GLA_DV), jnp.float32)
        yp, k1, v1, lf1, s1 = trunk_layer(yp, mk, mv, None, s0, **lw)
        fkp.append(k1); fvp.append(v1); flp.append(lf1); sgp.append(s1); mkp.append(mk); mvp.append(mv)
        ys, k2, v2, lf2, s2 = trunk_layer(ys, cache_mem_k[l], cache_mem_v[l],
                                          (cache_fox_k[l], cache_fox_v[l], cache_fox_logf[l]), state_gla[l], **lw)
        fks.append(k2); fvs.append(v2); fls.append(lf2); sgs.append(s2)
    return (yp, ys, jnp.stack(fkp), jnp.stack(fvp), jnp.stack(flp), jnp.stack(sgp), jnp.stack(mkp), jnp.stack(mvp),
            jnp.stack(fks), jnp.stack(fvs), jnp.stack(fls), jnp.stack(sgs))
```

```python
import functools

import numpy as np
import jax
import jax.numpy as jnp
from jax import lax
from jax.experimental import pallas as pl
from jax.experimental.pallas import tpu as pltpu

F32 = jnp.float32
BF16 = jnp.bfloat16

EPS = 1e-6
D_MODEL = 1024
CHUNK = 64
GLA_HEADS, GLA_DK, GLA_DV, GLA_RANK, GLA_TAU = 4, 64, 128, 16, 16.0
FOX_HEADS, FOX_DH = 8, 64
MEM_HEADS, MEM_DH, N_MEM = 4, 128, 256
N_BRANCH, BRANCH_W = 3, 512
PEER_HEADS, N_KEYS, PEER_TOPK, PEER_DHALF = 8, 128, 16, 128
N_EXPERTS = N_KEYS * N_KEYS

V7X_VMEM_LIMIT_BYTES = 56 * 1024 * 1024
MASKED = -1e30

O_GQ, O_GK, O_GV, O_GG, O_FQ, O_FK, O_FV, O_MQ, O_GATE, O_SM, W_COLS = (
    0, 256, 512, 1024, 1536, 2048, 2560, 3072, 3584, 6656, 6784)
SM_FF, SM_GA = 0, 8


def _params(*sem):
    return pltpu.CompilerParams(dimension_semantics=sem, vmem_limit_bytes=V7X_VMEM_LIMIT_BYTES)


def _dot(a, b):
    return jnp.dot(a, b, preferred_element_type=F32)


def _dot_nt(a, b):
    return lax.dot_general(a, b, (((1,), (1,)), ((), ())), preferred_element_type=F32)


def _dot_tn(a, b):
    return lax.dot_general(a, b, (((0,), (0,)), ((), ())), preferred_element_type=F32)


def _log_sigmoid(x):
    return jnp.minimum(x, 0.0) - jnp.log1p(jnp.exp(-jnp.abs(x)))


def _split3(x):
    hi = x.astype(BF16).astype(F32)
    r = x - hi
    mid = r.astype(BF16).astype(F32)
    return hi, mid, r - mid


def _head_norm(v, bd_ref, g):
    ms = _dot((v * v).astype(BF16), bd_ref[...])
    return v * lax.rsqrt(ms + EPS) * g


def _full(shape):
    nd = len(shape)
    return pl.BlockSpec(shape, lambda *_: (0,) * nd)


def _inproj_kernel(x_ref, g_ref, w_ref, wa2_ref, bsm_ref, ba_ref, gfq_ref, gfk_ref, gmq_ref, bd64_ref, bd128_ref,
                   gq_ref, gk_ref, gv_ref, la_ref, sg_ref, fq_ref, fk_ref, fv_ref, mq_ref, gate_ref, lsm_ref):
    x = x_ref[...]
    ms = jnp.mean(x * x, axis=-1, keepdims=True)
    h = (x * lax.rsqrt(ms + EPS) * g_ref[...]).astype(BF16)

    def seg(lo, hi):
        return _dot(h, w_ref[:, lo:hi])

    gq_ref[...] = seg(O_GQ, O_GK)
    gk_ref[...] = seg(O_GK, O_GV)
    gv_ref[...] = seg(O_GV, O_GG).astype(BF16)
    gg = seg(O_GG, O_FQ)
    sg_ref[...] = (gg * jax.nn.sigmoid(gg)).astype(BF16)
    fq_ref[...] = (_head_norm(seg(O_FQ, O_FK), bd64_ref, gfq_ref[...]) * (FOX_DH ** -0.5)).astype(BF16)
    fk_ref[...] = _head_norm(seg(O_FK, O_FV), bd64_ref, gfk_ref[...])
    fv_ref[...] = seg(O_FV, O_MQ)
    mq_ref[...] = (_head_norm(seg(O_MQ, O_GATE), bd128_ref, gmq_ref[...]) * (MEM_DH ** -0.5)).astype(BF16)
    for b in range(N_BRANCH):
        lo = O_GATE + b * D_MODEL
        gate_ref[:, b * D_MODEL:(b + 1) * D_MODEL] = jax.nn.sigmoid(seg(lo, lo + D_MODEL)).astype(BF16)
    sm = seg(O_SM, W_COLS)
    la_ref[...] = _log_sigmoid(_dot(sm.astype(BF16), wa2_ref[...]) + ba_ref[...]) * (1.0 / GLA_TAU)
    lsm_ref[...] = _log_sigmoid(sm + bsm_ref[...])


def _inproj(x, g_attn, w, wa2p, bsm, ba, gfq, gfk, gmq, bd64, bd128, tm):
    n = x.shape[0]
    widths = [(256, F32), (256, F32), (512, BF16), (256, F32), (512, BF16), (512, BF16), (512, F32), (512, F32),
              (512, BF16), (3 * D_MODEL, BF16), (128, F32)]
    row = lambda wd: pl.BlockSpec((tm, wd), lambda i: (i, 0))
    consts = [g_attn, w, wa2p, bsm, ba, gfq, gfk, gmq, bd64, bd128]
    return pl.pallas_call(
        _inproj_kernel,
        grid=(n // tm,),
        in_specs=[row(D_MODEL)] + [pl.BlockSpec(c.shape, lambda i: (0, 0), pipeline_mode=pl.Buffered(1)) for c in consts],
        out_specs=[row(wd) for wd, _ in widths],
        out_shape=[jax.ShapeDtypeStruct((n, wd), dt) for wd, dt in widths],
        compiler_params=_params("parallel"),
        name="inproj",
    )(x, *consts)


def _cumsum_kernel(lf_ref, ltri_ref, c_ref, caug_ref, carry_ref):
    @pl.when(pl.program_id(1) == 0)
    def _():
        carry_ref[...] = jnp.zeros_like(carry_ref)

    tc = lf_ref.shape[1]
    lane = lax.broadcasted_iota(jnp.int32, (tc, 128), 1)
    x = jnp.where(lane < FOX_HEADS, lf_ref[0], 0.0)
    hi, mid, lo = _split3(x)
    lt = ltri_ref[...]
    c = _dot(lt, hi.astype(BF16)) + _dot(lt, mid.astype(BF16)) + _dot(lt, lo.astype(BF16)) + carry_ref[...]
    carry_ref[...] = c[tc - 1:tc, :]
    c_ref[0] = c
    ch, cm, cl = _split3(c)
    ones = jnp.where((lane >= 24) & (lane < 32), 1.0, 0.0)
    caug_ref[0] = (ch + pltpu.roll(cm, 8, axis=1) + pltpu.roll(cl, 16, axis=1) + ones).astype(BF16)


def _cumsum(lf, tc):
    b, t, _ = lf.shape
    ltri = jnp.asarray(np.tril(np.ones((tc, tc), np.float32)), BF16)
    blk = pl.BlockSpec((1, tc, 128), lambda i, j: (i, j, 0))
    return pl.pallas_call(
        _cumsum_kernel,
        grid=(b, t // tc),
        in_specs=[blk, _full((tc, tc))],
        out_specs=[blk, blk],
        out_shape=[jax.ShapeDtypeStruct((b, t, 128), F32), jax.ShapeDtypeStruct((b, t, 128), BF16)],
        scratch_shapes=[pltpu.VMEM((1, 128), F32)],
        compiler_params=_params("parallel", "arbitrary"),
        name="cumsum",
    )(lf, ltri)


def _foxprep_kernel(fq_ref, fk_ref, fv_ref, c_ref, q_ref, k_ref, v_ref):
    tm = fq_ref.shape[0]
    lane = lax.broadcasted_iota(jnp.int32, (tm, 128), 1)
    ch, cm, cl = _split3(c_ref[...])
    fq = fq_ref[...].astype(F32)
    fk = fk_ref[...]
    fv = fv_ref[...]
    vb = jnp.where(lane == FOX_DH, 1.0, 0.0)
    for h in range(FOX_HEADS):
        col = lambda a: jnp.broadcast_to(a[:, h:h + 1], (tm, 128))
        bh, bm, bl = col(ch), col(cm), col(cl)
        qb = jnp.where(lane == 64, bh, jnp.where(lane == 65, bm, jnp.where(lane == 66, bl,
                       jnp.where((lane >= 67) & (lane < 70), 1.0, 0.0))))
        kb = jnp.where((lane >= 64) & (lane < 67), 1.0, jnp.where(lane == 67, -bh, jnp.where(lane == 68, -bm,
                       jnp.where(lane == 69, -bl, 0.0))))

        def pick(a):
            s = a[:, 128 * (h // 2):128 * (h // 2) + 128]
            return pltpu.roll(s, 64, axis=1) if h % 2 else s

        q_ref[0, h] = jnp.where(lane < FOX_DH, pick(fq), qb).astype(BF16)
        k_ref[0, h] = jnp.where(lane < FOX_DH, pick(fk), kb).astype(BF16)
        v_ref[0, h] = jnp.where(lane < FOX_DH, pick(fv), vb).astype(BF16)


def _foxprep(fq, fk, fv, c, b, t, tm):
    row = lambda wd: pl.BlockSpec((tm, wd), lambda i, j: (i * (t // tm) + j, 0))
    hm = pl.BlockSpec((1, FOX_HEADS, tm, 128), lambda i, j: (i, 0, j, 0))
    shp = jax.ShapeDtypeStruct((b, FOX_HEADS, t, 128), BF16)
    return pl.pallas_call(
        _foxprep_kernel,
        grid=(b, t // tm),
        in_specs=[row(512), row(512), row(512), row(128)],
        out_specs=[hm, hm, hm],
        out_shape=[shp, shp, shp],
        compiler_params=_params("parallel", "parallel"),
        name="foxprep",
    )(fq, fk, fv, c)


def _fox_prompt_kernel(q_ref, k_ref, v_ref, o_ref, *, tq):
    qi = pl.program_id(2)
    outs = []
    for hh in range(2):
        q = q_ref[0, hh]

        def update(carry, k, v, mask):
            m, acc = carry
            s = _dot_nt(q, k)
            if mask:
                rowi = lax.broadcasted_iota(jnp.int32, s.shape, 0)
                coli = lax.broadcasted_iota(jnp.int32, s.shape, 1)
                s = jnp.where(coli <= rowi, s, MASKED)
            m_new = jnp.maximum(m, jnp.max(s, axis=1, keepdims=True))
            p = jnp.exp(s - m_new)
            return m_new, acc * jnp.exp(m - m_new) + _dot(p.astype(BF16), v)

        def body(ki, carry):
            off = pl.multiple_of(ki * tq, tq)
            return update(carry, k_ref[0, hh, pl.ds(off, tq), :], v_ref[0, hh, pl.ds(off, tq), :], False)

        carry = (jnp.full((tq, 1), MASKED, F32), jnp.zeros((tq, 128), F32))
        carry = lax.fori_loop(0, qi, body, carry)
        off = pl.multiple_of(qi * tq, tq)
        _, acc = update(carry, k_ref[0, hh, pl.ds(off, tq), :], v_ref[0, hh, pl.ds(off, tq), :], True)
        outs.append(acc * (1.0 / acc[:, FOX_DH:FOX_DH + 1]))
    lane = lax.broadcasted_iota(jnp.int32, (tq, 128), 1)
    o_ref[0] = jnp.where(lane < FOX_DH, outs[0], pltpu.roll(outs[1], 64, axis=1)).astype(BF16)


def _fox_prompt(q, k, v, tq):
    b, _, t, _ = q.shape
    kv = pl.BlockSpec((1, 2, t, 128), lambda i, hp, j: (i, hp, 0, 0))
    return pl.pallas_call(
        functools.partial(_fox_prompt_kernel, tq=tq),
        grid=(b, FOX_HEADS // 2, t // tq),
        in_specs=[pl.BlockSpec((1, 2, tq, 128), lambda i, hp, j: (i, hp, j, 0)), kv, kv],
        out_specs=pl.BlockSpec((1, tq, 128), lambda i, hp, j: (i, j, hp)),
        out_shape=jax.ShapeDtypeStruct((b, t, BRANCH_W), BF16),
        compiler_params=_params("parallel", "parallel", "arbitrary"),
        name="fox_prompt",
    )(q, k, v)


def _fox_sample_kernel(q_ref, cq_ref, caugp_ref, caugn_ref, kp_ref, vp_ref, kn_ref, vn_ref, rsel_ref, o_ref,
                       qt_ref, m_ref, l_ref, acc_ref):
    kb = pl.program_id(1)
    last = pl.num_programs(1) - 1
    nq = q_ref.shape[1]
    hw = FOX_HEADS * nq

    @pl.when(kb == 0)
    def _():
        q = q_ref[0]
        lq = lax.broadcasted_iota(jnp.int32, q.shape, 1)
        la = lax.broadcasted_iota(jnp.int32, (nq, 128), 1)
        ch, cm, cl = _split3(cq_ref[0])
        for h in range(FOX_HEADS):
            col = lambda a: jnp.broadcast_to(a[:, h:h + 1], (nq, 128))
            aug = jnp.where((la == h) | (la == 8 + h) | (la == 16 + h), -1.0,
                            jnp.where(la == 24, col(ch), jnp.where(la == 25, col(cm), jnp.where(la == 26, col(cl), 0.0))))
            qt_ref[h * nq:(h + 1) * nq, 0:512] = jnp.where((lq >> 6) == h, q, jnp.zeros_like(q))
            qt_ref[h * nq:(h + 1) * nq, 512:640] = aug.astype(BF16)
        m_ref[...] = jnp.full_like(m_ref, MASKED)
        l_ref[...] = jnp.zeros_like(l_ref)
        acc_ref[...] = jnp.zeros_like(acc_ref)

    def step(k, v, caug, causal):
        s = _dot_nt(jnp.concatenate([k.astype(BF16), caug], axis=1), qt_ref[...])
        if causal:
            key = lax.broadcasted_iota(jnp.int32, s.shape, 0)
            qry = lax.broadcasted_iota(jnp.int32, s.shape, 1) & (nq - 1)
            s = jnp.where(key <= qry, s, MASKED)
        m_prev = m_ref[...]
        m_new = jnp.maximum(m_prev, jnp.max(s, axis=0, keepdims=True))
        alpha = jnp.exp(m_prev - m_new)
        p = jnp.exp(s - m_new)
        l_ref[...] = l_ref[...] * alpha + jnp.sum(p, axis=0, keepdims=True)
        acc_ref[...] = acc_ref[...] * alpha + _dot_tn(v.astype(BF16), p.astype(BF16))
        m_ref[...] = m_new

    @pl.when(kb < last)
    def _():
        step(kp_ref[0], vp_ref[0], caugp_ref[0], False)

    @pl.when(kb == last)
    def _():
        step(kn_ref[0], vn_ref[0], caugn_ref[0], True)
        r = lax.broadcasted_iota(jnp.int32, (512, hw), 0) >> 6
        c = lax.broadcasted_iota(jnp.int32, (512, hw), 1) >> 6
        own = jnp.where(r == c, acc_ref[...] * (1.0 / l_ref[...]), 0.0).astype(BF16)
        o_ref[0] = _dot_nt(rsel_ref[...], own).astype(BF16)


def _fox_sample(fq, c_all, caug, k_past, v_past, fk, fv, tk):
    b, nq, _ = fq.shape
    past = k_past.shape[1]
    nkb = past // tk
    assert nq == 64 and past % tk == 0 and past % nq == 0
    hw = FOX_HEADS * nq
    rsel = jnp.asarray((np.arange(hw)[None, :] % nq) == np.arange(nq)[:, None], BF16)
    pb = lambda wd: pl.BlockSpec((1, tk, wd), lambda i, j: (i, jnp.minimum(j, nkb - 1), 0))
    nb = lambda wd: pl.BlockSpec((1, nq, wd), lambda i, j: (i, 0, 0))
    tail = lambda wd: pl.BlockSpec((1, nq, wd), lambda i, j: (i, past // nq, 0))
    return pl.pallas_call(
        _fox_sample_kernel,
        grid=(b, nkb + 1),
        in_specs=[nb(512), tail(128), pb(128), tail(128), pb(512), pb(512), nb(512), nb(512), _full((nq, hw))],
        out_specs=nb(512),
        out_shape=jax.ShapeDtypeStruct((b, nq, BRANCH_W), BF16),
        scratch_shapes=[pltpu.VMEM((hw, 640), BF16), pltpu.VMEM((1, hw), F32), pltpu.VMEM((1, hw), F32),
                        pltpu.VMEM((512, hw), F32)],
        compiler_params=_params("parallel", "arbitrary"),
        name="fox_sample",
    )(fq, c_all, caug, caug, k_past, v_past, fk, fv, rsel)


def _gla_kernel(q_ref, k_ref, v_ref, la_ref, sg_ref, gout_ref, s0_ref, ltri_ref, o_ref, sfin_ref, s_ref):
    t = pl.program_id(1)

    @pl.when(t == 0)
    def _():
        s_ref[...] = s0_ref[0]

    n_chunks = q_ref.shape[0] // CHUNK
    ri = lax.broadcasted_iota(jnp.int32, (CHUNK, CHUNK), 0)
    ci = lax.broadcasted_iota(jnp.int32, (CHUNK, CHUNK), 1)
    eye = jnp.where(ri == ci, 1.0, 0.0)
    causal = ci <= ri
    lt = ltri_ref[...]
    gout = gout_ref[...]

    def chunk(c, _):
        rows = pl.ds(pl.multiple_of(c * CHUNK, CHUNK), CHUNK)
        la = la_ref[rows, :]
        la_hi = la.astype(BF16)
        b = _dot(lt, la_hi) + _dot(lt, (la - la_hi.astype(F32)).astype(BF16))
        b_last = b[CHUNK - 1:CHUNK, :]
        q = q_ref[rows, :]
        k = k_ref[rows, :]
        qd = (q * (GLA_DK ** -0.5) * jnp.exp(b)).astype(BF16)
        kd = (k * jnp.exp(-b)).astype(BF16)
        ko = (k * jnp.exp(b_last - b)).astype(BF16)
        dl = jnp.exp(b_last)
        v = v_ref[rows, :]
        outs = []
        for h in range(GLA_HEADS):
            ks = slice(h * GLA_DK, (h + 1) * GLA_DK)
            vs = slice(h * GLA_DV, (h + 1) * GLA_DV)
            att = jnp.where(causal, _dot_nt(qd[:, ks], kd[:, ks]), 0.0).astype(BF16)
            s_old = s_ref[h]
            o = _dot(att, v[:, vs]) + _dot(qd[:, ks], s_old.astype(BF16))
            dcol = jnp.sum(eye * dl[:, ks], axis=1, keepdims=True)
            s_ref[h] = s_old * dcol + _dot_tn(ko[:, ks], v[:, vs])
            ms = jnp.mean(o * o, axis=-1, keepdims=True)
            outs.append(o * lax.rsqrt(ms + EPS) * gout[:, vs])
        o_ref[rows, :] = (jnp.concatenate(outs, axis=1) * sg_ref[rows, :].astype(F32)).astype(BF16)
        return 0

    lax.fori_loop(0, n_chunks, chunk, 0)

    @pl.when(t == pl.num_programs(1) - 1)
    def _():
        sfin_ref[0] = s_ref[...]


def _gla(gq, gk, gv, la, sg, gout, s0, b, t, blk):
    row = lambda wd: pl.BlockSpec((blk, wd), lambda i, j: (i * (t // blk) + j, 0))
    st = pl.BlockSpec((1, GLA_HEADS, GLA_DK, GLA_DV), lambda i, j: (i, 0, 0, 0))
    ltri = jnp.asarray(np.tril(np.ones((CHUNK, CHUNK), np.float32)), BF16)
    return pl.pallas_call(
        _gla_kernel,
        grid=(b, t // blk),
        in_specs=[row(256), row(256), row(512), row(256), row(512), _full((1, 512)), st, _full((CHUNK, CHUNK))],
        out_specs=[row(512), st],
        out_shape=[jax.ShapeDtypeStruct((b * t, BRANCH_W), BF16),
                   jax.ShapeDtypeStruct((b, GLA_HEADS, GLA_DK, GLA_DV), F32)],
        scratch_shapes=[pltpu.VMEM((GLA_HEADS, GLA_DK, GLA_DV), F32)],
        compiler_params=_params("parallel", "arbitrary"),
        name="gla",
    )(gq, gk, gv, la, sg, gout, s0, ltri)


def _memkv_kernel(x_ref, g_ref, w_ref, gk_ref, bd128_ref, k_ref, v_ref):
    x = x_ref[...]
    ms = jnp.mean(x * x, axis=-1, keepdims=True)
    h = (x * lax.rsqrt(ms + EPS) * g_ref[...]).astype(BF16)
    kv = _dot(h, w_ref[...])
    k_ref[...] = _head_norm(kv[:, :512], bd128_ref, gk_ref[...])
    v_ref[...] = kv[:, 512:]


def _memkv(mem, g_in, w_kv, gk, bd128, tm):
    n = mem.shape[0]
    row = lambda wd: pl.BlockSpec((tm, wd), lambda i: (i, 0))
    return pl.pallas_call(
        _memkv_kernel,
        grid=(n // tm,),
        in_specs=[row(D_MODEL), _full((1, D_MODEL)), _full(w_kv.shape), _full((1, 512)), _full((512, 512))],
        out_specs=[row(512), row(512)],
        out_shape=[jax.ShapeDtypeStruct((n, 512), F32)] * 2,
        compiler_params=_params("parallel"),
        name="memkv",
    )(mem, g_in, w_kv, gk, bd128)


def _merge_kernel(x_ref, og_ref, of_ref, mq_ref, mk_ref, mv_ref, gate_ref, wbr_ref, wout_ref, gffn_ref,
                  x2_ref, h2_ref):
    mq = mq_ref[...]
    mk = mk_ref[0].astype(BF16)
    mv = mv_ref[0].astype(BF16)
    om = []
    for h in range(MEM_HEADS):
        hs = slice(h * MEM_DH, (h + 1) * MEM_DH)
        s = _dot_nt(mq[:, hs], mk[:, hs])
        e = jnp.exp(s - jnp.max(s, axis=1, keepdims=True))
        om.append(_dot(e.astype(BF16), mv[:, hs]) * (1.0 / jnp.sum(e, axis=1, keepdims=True)))
    branches = (og_ref[...], of_ref[...], jnp.concatenate(om, axis=1).astype(BF16))
    mix = None
    for b in range(N_BRANCH):
        term = gate_ref[:, b * D_MODEL:(b + 1) * D_MODEL].astype(F32) * _dot(branches[b], wbr_ref[b])
        mix = term if mix is None else mix + term
    x2 = x_ref[...] + _dot(mix.astype(BF16), wout_ref[...])
    x2_ref[...] = x2
    ms = jnp.mean(x2 * x2, axis=-1, keepdims=True)
    h2_ref[...] = (x2 * lax.rsqrt(ms + EPS) * gffn_ref[...]).astype(BF16)


def _merge(x, og, of, mq, mk, mv, gate, wbr, wout, gffn, t, tm):
    n = x.shape[0]
    row = lambda wd: pl.BlockSpec((tm, wd), lambda i: (i, 0))
    mem = pl.BlockSpec((1, N_MEM, 512), lambda i: (i // (t // tm), 0, 0))
    return pl.pallas_call(
        _merge_kernel,
        grid=(n // tm,),
        in_specs=[row(D_MODEL), row(512), row(512), row(512), mem, mem, row(3 * D_MODEL),
                  _full(wbr.shape), _full(wout.shape), _full((1, D_MODEL))],
        out_specs=[row(D_MODEL), row(D_MODEL)],
        out_shape=[jax.ShapeDtypeStruct((n, D_MODEL), F32), jax.ShapeDtypeStruct((n, D_MODEL), BF16)],
        compiler_params=_params("parallel"),
        name="merge",
    )(x, og, of, mq, mk, mv, gate, wbr, wout, gffn)


def _extract_top(s, k, want_mask):
    rowi = lax.broadcasted_iota(jnp.int32, s.shape, 0)
    cur, vals, idxs = s, [], []
    taken = jnp.zeros(s.shape, F32) if want_mask else None
    for _ in range(k):
        m = jnp.max(cur, axis=0, keepdims=True)
        idx = jnp.min(jnp.where(cur == m, rowi, s.shape[0]), axis=0, keepdims=True)
        hit = rowi == idx
        cur = jnp.where(hit, -jnp.inf, cur)
        if want_mask:
            taken = jnp.where(hit, 1.0, taken)
        vals.append(m)
        idxs.append(idx)
    return vals, idxs, taken


def _peer_select_kernel(h_ref, wqt_ref, sk_ref, e1_ref, e2_ref, n1_ref, rk_ref):
    hb = h_ref[...]
    tm = hb.shape[0]
    rowi = lax.broadcasted_iota(jnp.int32, (N_KEYS, tm), 0)

    def head(h, _):
        def scores(p):
            r = h * 2 + p
            q_t = _dot_nt(wqt_ref[pl.ds(pl.multiple_of(r * PEER_DHALF, PEER_DHALF), PEER_DHALF), :], hb)
            return _dot(sk_ref[r], q_t.astype(BF16))

        s1, s2 = scores(0), scores(1)
        v1, i1, _ = _extract_top(s1, PEER_TOPK, False)
        v2, i2, _ = _extract_top(s2, PEER_TOPK, False)
        v2_all = jnp.concatenate(v2, axis=0)
        cand = jnp.concatenate([v1[a] + v2_all for a in range(PEER_TOPK)], axis=0)
        _, _, sel = _extract_top(cand, PEER_TOPK, True)
        e = sel * jnp.exp(cand - (v1[0] + v2[0]))
        inv_z = 1.0 / jnp.sum(e, axis=0, keepdims=True)
        n_a = jnp.sum(sel.reshape(PEER_TOPK, PEER_TOPK, tm), axis=1)
        n1 = jnp.zeros((N_KEYS, tm), F32)
        rk = jnp.full((N_KEYS, tm), float(N_KEYS - 1), F32)
        for a in range(PEER_TOPK):
            n1 = jnp.where(rowi == i1[a], n_a[a:a + 1, :], n1)
            rk = jnp.where(rowi == i2[a], float(a), rk)
        e1_ref[h] = jnp.exp(s1 - v1[0]) * inv_z
        e2_ref[h] = jnp.exp(s2 - v2[0])
        n1_ref[h] = n1
        rk_ref[h] = rk
        return 0

    lax.fori_loop(0, PEER_HEADS, head, 0)


def _peer_select(h2, wqt, sk, tm):
    n = h2.shape[0]
    tab = pl.BlockSpec((PEER_HEADS, N_KEYS, tm), lambda i: (0, 0, i))
    shp = jax.ShapeDtypeStruct((PEER_HEADS, N_KEYS, n), F32)
    return pl.pallas_call(
        _peer_select_kernel,
        grid=(n // tm,),
        in_specs=[pl.BlockSpec((tm, D_MODEL), lambda i: (i, 0)), _full(wqt.shape), _full(sk.shape)],
        out_specs=[tab] * 4,
        out_shape=[shp] * 4,
        compiler_params=_params("parallel"),
        name="peer_select",
    )(h2, wqt, sk)


def _peer_ffn_kernel(x2_ref, h_ref, e1_ref, e2_ref, n1_ref, rk_ref, u_ref, v_ref, y_ref, wt_ref, acc_ref):
    e = pl.program_id(1)

    @pl.when(e == 0)
    def _():
        acc_ref[...] = jnp.zeros_like(acc_ref)

    eb = u_ref.shape[0]
    a_t = _dot_nt(u_ref[...], h_ref[...])
    for ii in range(eb // N_KEYS):
        i = e * (eb // N_KEYS) + ii
        a = a_t[ii * N_KEYS:(ii + 1) * N_KEYS, :]
        gate = None
        for h in range(PEER_HEADS):
            term = jnp.where(rk_ref[h] < n1_ref[h, pl.ds(i, 1), :], e2_ref[h], 0.0) * e1_ref[h, pl.ds(i, 1), :]
            gate = term if gate is None else gate + term
        gelu = 0.5 * a * (1.0 + lax.erf(a * (2.0 ** -0.5)))
        wt_ref[ii * N_KEYS:(ii + 1) * N_KEYS, :] = (gate * gelu).astype(BF16)
    acc_ref[...] += _dot_tn(wt_ref[...], v_ref[...])

    @pl.when(e == pl.num_programs(1) - 1)
    def _():
        y_ref[...] = x2_ref[...] + acc_ref[...]


def _peer_ffn(x2, h2, e1, e2, n1, rk, u, v, tm, eb):
    n = x2.shape[0]
    row = lambda: pl.BlockSpec((tm, D_MODEL), lambda i, j: (i, 0))
    tab = pl.BlockSpec((PEER_HEADS, N_KEYS, tm), lambda i, j: (0, 0, i))
    ex = pl.BlockSpec((eb, D_MODEL), lambda i, j: (j, 0))
    return pl.pallas_call(
        _peer_ffn_kernel,
        grid=(n // tm, N_EXPERTS // eb),
        in_specs=[row(), row(), tab, tab, tab, tab, ex, ex],
        out_specs=row(),
        out_shape=jax.ShapeDtypeStruct((n, D_MODEL), F32),
        scratch_shapes=[pltpu.VMEM((eb, tm), BF16), pltpu.VMEM((tm, D_MODEL), F32)],
        compiler_params=_params("parallel", "arbitrary"),
        name="peer_ffn",
    )(x2, h2, e1, e2, n1, rk, u, v)


def _block_diag_mean(width, head):
    i = np.arange(width)
    return jnp.asarray((i[:, None] // head == i[None, :] // head) / head, BF16)


def _tile(n, pref):
    return pref if n % pref == 0 else n


def _cumsum_tile(t):
    return max(d for d in range(16, min(t, 1024) + 1, 16) if t % d == 0)


def _prepare_weights(l, g_attn, w_in, gla_w_a2, gla_b_a, gla_g_out, fox_b_f, fox_g_q, fox_g_k, mem_g_q, mem_g_in,
                     mem_w_kv, mem_g_k, w_branch, w_out, g_ffn, peer_w_q, peer_sub_keys, peer_u, peer_v):
    w = w_in[l]
    wr = jnp.concatenate([w[:, 0:1536], w[:, 1552:3088], w[:, 3096:6680], w[:, 3088:3096], w[:, 1536:1552],
                          jnp.zeros((D_MODEL, W_COLS - 6680), F32)], axis=1).astype(BF16)
    wa2p = jnp.zeros((128, GLA_HEADS * GLA_DK), F32).at[SM_GA:SM_GA + GLA_RANK].set(gla_w_a2[l]).astype(BF16)
    bsm = jnp.zeros((1, 128), F32).at[0, SM_FF:SM_FF + FOX_HEADS].set(fox_b_f[l])
    return dict(
        g_attn=g_attn[l][None], w=wr, wa2p=wa2p, bsm=bsm, ba=gla_b_a[l][None],
        gfq=jnp.tile(fox_g_q[l], FOX_HEADS)[None], gfk=jnp.tile(fox_g_k[l], FOX_HEADS)[None],
        gmq=jnp.tile(mem_g_q[l], MEM_HEADS)[None], gout=jnp.tile(gla_g_out[l], GLA_HEADS)[None],
        bd64=_block_diag_mean(512, FOX_DH), bd128=_block_diag_mean(512, MEM_DH),
        mem_g_in=mem_g_in[l][None], mem_w_kv=mem_w_kv[l].astype(BF16), mem_g_k=jnp.tile(mem_g_k[l], MEM_HEADS)[None],
        wbr=w_branch[l].astype(BF16), wout=w_out[l].astype(BF16), gffn=g_ffn[l][None],
        wqt=peer_w_q[l].T.astype(BF16),
        sk=peer_sub_keys[l].reshape(PEER_HEADS * 2, N_KEYS, PEER_DHALF).astype(BF16),
        u=peer_u[l].astype(BF16), v=peer_v[l].astype(BF16))


def _trunk(x, mem_k, mem_v, fox_past, s0, p):
    b, t, _ = x.shape
    n = b * t
    xf = x.reshape(n, D_MODEL)
    gq, gk, gv, la, sg, fq, fk, fv, mq, gate, lsm = _inproj(
        xf, p["g_attn"], p["w"], p["wa2p"], p["bsm"], p["ba"], p["gfq"], p["gfk"], p["gmq"], p["bd64"], p["bd128"],
        _tile(n, 256))
    o_gla, s_new = _gla(gq, gk, gv, la, sg, p["gout"], s0, b, t, _tile(t, 512))
    if fox_past is None:
        c, _ = _cumsum(lsm.reshape(b, t, 128), _cumsum_tile(t))
        qh, kh, vh = _foxprep(fq, fk, fv, c.reshape(n, 128), b, t, _tile(t, 256))
        o_fox = _fox_prompt(qh, kh, vh, _tile(t, 512)).reshape(n, BRANCH_W)
    else:
        k_past, v_past, lf_past = fox_past
        past = k_past.shape[1]
        lf = jnp.concatenate([jnp.pad(lf_past, ((0, 0), (0, 0), (0, 128 - FOX_HEADS))), lsm.reshape(b, t, 128)], axis=1)
        c_all, caug = _cumsum(lf, _cumsum_tile(past + t))
        o_fox = _fox_sample(fq.reshape(b, t, 512), c_all, caug, k_past, v_past, fk.reshape(b, t, 512),
                            fv.reshape(b, t, 512), _tile(past, 512)).reshape(n, BRANCH_W)
    x2, h2 = _merge(xf, o_gla, o_fox, mq, mem_k, mem_v, gate, p["wbr"], p["wout"], p["gffn"], t, _tile(t, 256))
    e1, e2, n1, rk = _peer_select(h2, p["wqt"], p["sk"], _tile(n, 256))
    y = _peer_ffn(x2, h2, e1, e2, n1, rk, p["u"], p["v"], _tile(n, 512), 1024)
    logf = lsm[:, SM_FF:SM_FF + FOX_HEADS]
    return (y.reshape(b, t, D_MODEL), fk.reshape(b, t, FOX_HEADS, FOX_DH), fv.reshape(b, t, FOX_HEADS, FOX_DH),
            logf.reshape(b, t, FOX_HEADS), s_new)


def kernel(x_prompt, x_sample, mem_prompt, cache_fox_k, cache_fox_v, cache_fox_logf, state_gla, cache_mem_k, cache_mem_v, g_attn, w_in, gla_w_a2, gla_b_a, gla_g_out, fox_b_f, fox_g_q, fox_g_k, mem_g_q, mem_g_in, mem_w_kv, mem_g_k, w_branch, w_out, g_ffn, peer_w_q, peer_sub_keys, peer_u, peer_v):
    depth = w_in.shape[0]
    yp, ys = x_prompt, x_sample
    bp, bs = x_prompt.shape[0], x_sample.shape[0]
    outs = [[] for _ in range(10)]
    for l in range(depth):
        p = _prepare_weights(l, g_attn, w_in, gla_w_a2, gla_b_a, gla_g_out, fox_b_f, fox_g_q, fox_g_k, mem_g_q,
                             mem_g_in, mem_w_kv, mem_g_k, w_branch, w_out, g_ffn, peer_w_q, peer_sub_keys, peer_u,
                             peer_v)
        mk, mv = _memkv(mem_prompt.reshape(bp * N_MEM, D_MODEL), p["mem_g_in"], p["mem_w_kv"], p["mem_g_k"],
                        p["bd128"], _tile(bp * N_MEM, 256))
        mk = mk.reshape(bp, N_MEM, 512)
        mv = mv.reshape(bp, N_MEM, 512)
        s0 = jnp.zeros((bp, GLA_HEADS, GLA_DK, GLA_DV), F32)
        yp, k1, v1, lf1, s1 = _trunk(yp, mk, mv, None, s0, p)
        past = cache_fox_k.shape[2]
        ys, k2, v2, lf2, s2 = _trunk(
            ys, cache_mem_k[l].reshape(bs, N_MEM, 512), cache_mem_v[l].reshape(bs, N_MEM, 512),
            (cache_fox_k[l].reshape(bs, past, 512), cache_fox_v[l].reshape(bs, past, 512), cache_fox_logf[l]),
            state_gla[l], p)
        for lst, val in zip(outs, (k1, v1, lf1, s1, mk.reshape(bp, N_MEM, MEM_HEADS, MEM_DH),
                                   mv.reshape(bp, N_MEM, MEM_HEADS, MEM_DH), k2, v2, lf2, s2)):
            lst.append(val)
    return (yp, ys) + tuple(jnp.stack(o) for o in outs)
```

```python
import functools

import numpy as np
import jax
import jax.numpy as jnp
from jax import lax
from jax.experimental import pallas as pl
from jax.experimental.pallas import tpu as pltpu

F32 = jnp.float32
BF16 = jnp.bfloat16

EPS = 1e-6
D_MODEL = 1024
CHUNK = 64
GLA_HEADS, GLA_DK, GLA_DV, GLA_RANK, GLA_TAU = 4, 64, 128, 16, 16.0
FOX_HEADS, FOX_DH = 8, 64
MEM_HEADS, MEM_DH, N_MEM = 4, 128, 256
N_BRANCH, BRANCH_W = 3, 512
PEER_HEADS, N_KEYS, PEER_TOPK, PEER_DHALF = 8, 128, 16, 128
N_EXPERTS = N_KEYS * N_KEYS

V7X_VMEM_LIMIT_BYTES = 56 * 1024 * 1024
MASKED = -1e30

O_GQ, O_GK, O_GV, O_GG, O_FQ, O_FK, O_FV, O_MQ, O_GATE, O_SM, W_COLS = (
    0, 256, 512, 1024, 1536, 2048, 2560, 3072, 3584, 6656, 6784)
SM_FF, SM_GA = 0, 8


def _params(*sem):
    return pltpu.CompilerParams(dimension_semantics=sem, vmem_limit_bytes=V7X_VMEM_LIMIT_BYTES)


def _dot(a, b):
    return jnp.dot(a, b, preferred_element_type=F32)


def _dot_nt(a, b):
    return lax.dot_general(a, b, (((1,), (1,)), ((), ())), preferred_element_type=F32)


def _dot_tn(a, b):
    return lax.dot_general(a, b, (((0,), (0,)), ((), ())), preferred_element_type=F32)


def _log_sigmoid(x):
    return jnp.minimum(x, 0.0) - jnp.log1p(jnp.exp(-jnp.abs(x)))


def _split3(x):
    hi = x.astype(BF16).astype(F32)
    r = x - hi
    mid = r.astype(BF16).astype(F32)
    return hi, mid, r - mid


def _head_norm(v, bd_ref, g):
    ms = _dot((v * v).astype(BF16), bd_ref[...])
    return v * lax.rsqrt(ms + EPS) * g


def _full(shape):
    nd = len(shape)
    return pl.BlockSpec(shape, lambda *_: (0,) * nd)


def _inproj_kernel(x_ref, g_ref, w_ref, wa2_ref, bsm_ref, ba_ref, gfq_ref, gfk_ref, gmq_ref, bd64_ref, bd128_ref,
                   gq_ref, gk_ref, gv_ref, la_ref, sg_ref, fq_ref, fk_ref, fv_ref, mq_ref, gate_ref, lsm_ref):
    x = x_ref[...]
    ms = jnp.mean(x * x, axis=-1, keepdims=True)
    h = (x * lax.rsqrt(ms + EPS) * g_ref[...]).astype(BF16)

    def seg(lo, hi):
        return _dot(h, w_ref[:, lo:hi])

    gq_ref[...] = seg(O_GQ, O_GK)
    gk_ref[...] = seg(O_GK, O_GV)
    gv_ref[...] = seg(O_GV, O_GG).astype(BF16)
    gg = seg(O_GG, O_FQ)
    sg_ref[...] = (gg * jax.nn.sigmoid(gg)).astype(BF16)
    fq_ref[...] = (_head_norm(seg(O_FQ, O_FK), bd64_ref, gfq_ref[...]) * (FOX_DH ** -0.5)).astype(BF16)
    fk_ref[...] = _head_norm(seg(O_FK, O_FV), bd64_ref, gfk_ref[...])
    fv_ref[...] = seg(O_FV, O_MQ)
    mq_ref[...] = (_head_norm(seg(O_MQ, O_GATE), bd128_ref, gmq_ref[...]) * (MEM_DH ** -0.5)).astype(BF16)
    for b in range(N_BRANCH):
        lo = O_GATE + b * D_MODEL
        gate_ref[:, b * D_MODEL:(b + 1) * D_MODEL] = jax.nn.sigmoid(seg(lo, lo + D_MODEL)).astype(BF16)
    sm = seg(O_SM, W_COLS)
    la_ref[...] = _log_sigmoid(_dot(sm.astype(BF16), wa2_ref[...]) + ba_ref[...]) * (1.0 / GLA_TAU)
    lsm_ref[...] = _log_sigmoid(sm + bsm_ref[...])


def _inproj(x, g_attn, w, wa2p, bsm, ba, gfq, gfk, gmq, bd64, bd128, tm):
    n = x.shape[0]
    widths = [(256, F32), (256, F32), (512, BF16), (256, F32), (512, BF16), (512, BF16), (512, F32), (512, F32),
              (512, BF16), (3 * D_MODEL, BF16), (128, F32)]
    row = lambda wd: pl.BlockSpec((tm, wd), lambda i: (i, 0))
    consts = [g_attn, w, wa2p, bsm, ba, gfq, gfk, gmq, bd64, bd128]
    return pl.pallas_call(
        _inproj_kernel,
        grid=(n // tm,),
        in_specs=[row(D_MODEL)] + [pl.BlockSpec(c.shape, lambda i: (0, 0), pipeline_mode=pl.Buffered(1)) for c in consts],
        out_specs=[row(wd) for wd, _ in widths],
        out_shape=[jax.ShapeDtypeStruct((n, wd), dt) for wd, dt in widths],
        compiler_params=_params("parallel"),
        name="inproj",
    )(x, *consts)


def _cumsum_kernel(lf_ref, ltri_ref, c_ref, caug_ref, carry_ref):
    @pl.when(pl.program_id(1) == 0)
    def _():
        carry_ref[...] = jnp.zeros_like(carry_ref)

    tc = lf_ref.shape[1]
    lane = lax.broadcasted_iota(jnp.int32, (tc, 128), 1)
    x = jnp.where(lane < FOX_HEADS, lf_ref[0], 0.0)
    hi, mid, lo = _split3(x)
    lt = ltri_ref[...]
    c = _dot(lt, hi.astype(BF16)) + _dot(lt, mid.astype(BF16)) + _dot(lt, lo.astype(BF16)) + carry_ref[...]
    carry_ref[...] = c[tc - 1:tc, :]
    c_ref[0] = c
    ch, cm, cl = _split3(c)
    ones = jnp.where((lane >= 24) & (lane < 32), 1.0, 0.0)
    caug_ref[0] = (ch + pltpu.roll(cm, 8, axis=1) + pltpu.roll(cl, 16, axis=1) + ones).astype(BF16)


def _cumsum(lf, tc):
    b, t, _ = lf.shape
    ltri = jnp.asarray(np.tril(np.ones((tc, tc), np.float32)), BF16)
    blk = pl.BlockSpec((1, tc, 128), lambda i, j: (i, j, 0))
    return pl.pallas_call(
        _cumsum_kernel,
        grid=(b, t // tc),
        in_specs=[blk, _full((tc, tc))],
        out_specs=[blk, blk],
        out_shape=[jax.ShapeDtypeStruct((b, t, 128), F32), jax.ShapeDtypeStruct((b, t, 128), BF16)],
        scratch_shapes=[pltpu.VMEM((1, 128), F32)],
        compiler_params=_params("parallel", "arbitrary"),
        name="cumsum",
    )(lf, ltri)


def _foxprep_kernel(fq_ref, fk_ref, fv_ref, c_ref, q_ref, k_ref, v_ref):
    tm = fq_ref.shape[0]
    lane = lax.broadcasted_iota(jnp.int32, (tm, 128), 1)
    ch, cm, cl = _split3(c_ref[...])
    fq = fq_ref[...].astype(F32)
    fk = fk_ref[...]
    fv = fv_ref[...]
    vb = jnp.where(lane == FOX_DH, 1.0, 0.0)
    for h in range(FOX_HEADS):
        col = lambda a: jnp.broadcast_to(a[:, h:h + 1], (tm, 128))
        bh, bm, bl = col(ch), col(cm), col(cl)
        qb = jnp.where(lane == 64, bh, jnp.where(lane == 65, bm, jnp.where(lane == 66, bl,
                       jnp.where((lane >= 67) & (lane < 70), 1.0, 0.0))))
        kb = jnp.where((lane >= 64) & (lane < 67), 1.0, jnp.where(lane == 67, -bh, jnp.where(lane == 68, -bm,
                       jnp.where(lane == 69, -bl, 0.0))))

        def pick(a):
            s = a[:, 128 * (h // 2):128 * (h // 2) + 128]
            return pltpu.roll(s, 64, axis=1) if h % 2 else s

        q_ref[0, h] = jnp.where(lane < FOX_DH, pick(fq), qb).astype(BF16)
        k_ref[0, h] = jnp.where(lane < FOX_DH, pick(fk), kb).astype(BF16)
        v_ref[0, h] = jnp.where(lane < FOX_DH, pick(fv), vb).astype(BF16)


def _foxprep(fq, fk, fv, c, b, t, tm):
    row = lambda wd: pl.BlockSpec((tm, wd), lambda i, j: (i * (t // tm) + j, 0))
    hm = pl.BlockSpec((1, FOX_HEADS, tm, 128), lambda i, j: (i, 0, j, 0))
    shp = jax.ShapeDtypeStruct((b, FOX_HEADS, t, 128), BF16)
    return pl.pallas_call(
        _foxprep_kernel,
        grid=(b, t // tm),
        in_specs=[row(512), row(512), row(512), row(128)],
        out_specs=[hm, hm, hm],
        out_shape=[shp, shp, shp],
        compiler_params=_params("parallel", "parallel"),
        name="foxprep",
    )(fq, fk, fv, c)


def _fox_prompt_kernel(q_ref, k_ref, v_ref, o_ref, *, tq):
    qi = pl.program_id(2)
    def update(hh, carry, off, mask):
        m, acc = carry
        s = _dot_nt(q_ref[0, hh], k_ref[0, hh, pl.ds(off, tq), :])
        if mask:
            rowi = lax.broadcasted_iota(jnp.int32, s.shape, 0)
            coli = lax.broadcasted_iota(jnp.int32, s.shape, 1)
            s = jnp.where(coli <= rowi, s, MASKED)
        m_new = jnp.maximum(m, jnp.max(s, axis=1, keepdims=True))
        p = jnp.exp(s - m_new)
        return m_new, acc * jnp.exp(m - m_new) + _dot(p.astype(BF16), v_ref[0, hh, pl.ds(off, tq), :])

    def both(carry, off, mask):
        return tuple(update(hh, carry[hh], off, mask) for hh in range(2))

    init = (jnp.full((tq, 1), MASKED, F32), jnp.zeros((tq, 128), F32))
    carry = lax.fori_loop(0, qi, lambda ki, c: both(c, pl.multiple_of(ki * tq, tq), False), (init, init))
    carry = both(carry, pl.multiple_of(qi * tq, tq), True)
    outs = [acc * (1.0 / acc[:, FOX_DH:FOX_DH + 1]) for _, acc in carry]
    lane = lax.broadcasted_iota(jnp.int32, (tq, 128), 1)
    o_ref[0] = jnp.where(lane < FOX_DH, outs[0], pltpu.roll(outs[1], 64, axis=1)).astype(BF16)


def _fox_prompt(q, k, v, tq):
    b, _, t, _ = q.shape
    kv = pl.BlockSpec((1, 2, t, 128), lambda i, hp, j: (i, hp, 0, 0))
    return pl.pallas_call(
        functools.partial(_fox_prompt_kernel, tq=tq),
        grid=(b, FOX_HEADS // 2, t // tq),
        in_specs=[pl.BlockSpec((1, 2, tq, 128), lambda i, hp, j: (i, hp, j, 0)), kv, kv],
        out_specs=pl.BlockSpec((1, tq, 128), lambda i, hp, j: (i, j, hp)),
        out_shape=jax.ShapeDtypeStruct((b, t, BRANCH_W), BF16),
        compiler_params=_params("parallel", "parallel", "arbitrary"),
        name="fox_prompt",
    )(q, k, v)


def _fox_sample_kernel(q_ref, cq_ref, caugp_ref, caugn_ref, kp_ref, vp_ref, kn_ref, vn_ref, rsel_ref, o_ref,
                       qt_ref, m_ref, l_ref, acc_ref):
    kb = pl.program_id(1)
    last = pl.num_programs(1) - 1
    nq = q_ref.shape[1]
    hw = FOX_HEADS * nq

    @pl.when(kb == 0)
    def _():
        q = q_ref[0]
        lq = lax.broadcasted_iota(jnp.int32, q.shape, 1)
        la = lax.broadcasted_iota(jnp.int32, (nq, 128), 1)
        ch, cm, cl = _split3(cq_ref[0])
        for h in range(FOX_HEADS):
            col = lambda a: jnp.broadcast_to(a[:, h:h + 1], (nq, 128))
            aug = jnp.where((la == h) | (la == 8 + h) | (la == 16 + h), -1.0,
                            jnp.where(la == 24, col(ch), jnp.where(la == 25, col(cm), jnp.where(la == 26, col(cl), 0.0))))
            qt_ref[h * nq:(h + 1) * nq, 0:512] = jnp.where((lq >> 6) == h, q, jnp.zeros_like(q))
            qt_ref[h * nq:(h + 1) * nq, 512:640] = aug.astype(BF16)
        m_ref[...] = jnp.full_like(m_ref, MASKED)
        l_ref[...] = jnp.zeros_like(l_ref)
        acc_ref[...] = jnp.zeros_like(acc_ref)

    def step(k, v, caug, causal):
        s = _dot_nt(jnp.concatenate([k.astype(BF16), caug], axis=1), qt_ref[...])
        if causal:
            key = lax.broadcasted_iota(jnp.int32, s.shape, 0)
            qry = lax.broadcasted_iota(jnp.int32, s.shape, 1) & (nq - 1)
            s = jnp.where(key <= qry, s, MASKED)
        m_prev = m_ref[...]
        m_new = jnp.maximum(m_prev, jnp.max(s, axis=0, keepdims=True))
        alpha = jnp.exp(m_prev - m_new)
        p = jnp.exp(s - m_new)
        l_ref[...] = l_ref[...] * alpha + jnp.sum(p, axis=0, keepdims=True)
        acc_ref[...] = acc_ref[...] * alpha + _dot_tn(v.astype(BF16), p.astype(BF16))
        m_ref[...] = m_new

    @pl.when(kb < last)
    def _():
        step(kp_ref[0], vp_ref[0], caugp_ref[0], False)

    @pl.when(kb == last)
    def _():
        step(kn_ref[0], vn_ref[0], caugn_ref[0], True)
        r = lax.broadcasted_iota(jnp.int32, (512, hw), 0) >> 6
        c = lax.broadcasted_iota(jnp.int32, (512, hw), 1) >> 6
        own = jnp.where(r == c, acc_ref[...] * (1.0 / l_ref[...]), 0.0).astype(BF16)
        o_ref[0] = _dot_nt(rsel_ref[...], own).astype(BF16)


def _fox_sample(fq, c_all, caug, k_past, v_past, fk, fv, tk):
    b, nq, _ = fq.shape
    past = k_past.shape[1]
    nkb = past // tk
    assert nq == 64 and past % tk == 0 and past % nq == 0
    hw = FOX_HEADS * nq
    rsel = jnp.asarray((np.arange(hw)[None, :] % nq) == np.arange(nq)[:, None], BF16)
    pb = lambda wd: pl.BlockSpec((1, tk, wd), lambda i, j: (i, jnp.minimum(j, nkb - 1), 0))
    nb = lambda wd: pl.BlockSpec((1, nq, wd), lambda i, j: (i, 0, 0))
    tail = lambda wd: pl.BlockSpec((1, nq, wd), lambda i, j: (i, past // nq, 0))
    return pl.pallas_call(
        _fox_sample_kernel,
        grid=(b, nkb + 1),
        in_specs=[nb(512), tail(128), pb(128), tail(128), pb(512), pb(512), nb(512), nb(512), _full((nq, hw))],
        out_specs=nb(512),
        out_shape=jax.ShapeDtypeStruct((b, nq, BRANCH_W), BF16),
        scratch_shapes=[pltpu.VMEM((hw, 640), BF16), pltpu.VMEM((1, hw), F32), pltpu.VMEM((1, hw), F32),
                        pltpu.VMEM((512, hw), F32)],
        compiler_params=_params("parallel", "arbitrary"),
        name="fox_sample",
    )(fq, c_all, caug, caug, k_past, v_past, fk, fv, rsel)


def _gla_kernel(q_ref, k_ref, v_ref, la_ref, sg_ref, gout_ref, s0_ref, ltri_ref, o_ref, sfin_ref, s_ref):
    t = pl.program_id(1)

    @pl.when(t == 0)
    def _():
        s_ref[...] = s0_ref[0]

    n_chunks = q_ref.shape[0] // CHUNK
    ri = lax.broadcasted_iota(jnp.int32, (CHUNK, CHUNK), 0)
    ci = lax.broadcasted_iota(jnp.int32, (CHUNK, CHUNK), 1)
    eye = jnp.where(ri == ci, 1.0, 0.0)
    causal = ci <= ri
    lt = ltri_ref[...]
    gout = gout_ref[...]

    def chunk(c, _):
        rows = pl.ds(pl.multiple_of(c * CHUNK, CHUNK), CHUNK)
        la = la_ref[rows, :]
        la_hi = la.astype(BF16)
        b = _dot(lt, la_hi) + _dot(lt, (la - la_hi.astype(F32)).astype(BF16))
        b_last = b[CHUNK - 1:CHUNK, :]
        q = q_ref[rows, :]
        k = k_ref[rows, :]
        qd = (q * (GLA_DK ** -0.5) * jnp.exp(b)).astype(BF16)
        kd = (k * jnp.exp(-b)).astype(BF16)
        ko = (k * jnp.exp(b_last - b)).astype(BF16)
        dl = jnp.exp(b_last)
        v = v_ref[rows, :]
        outs = []
        for h in range(GLA_HEADS):
            ks = slice(h * GLA_DK, (h + 1) * GLA_DK)
            vs = slice(h * GLA_DV, (h + 1) * GLA_DV)
            att = jnp.where(causal, _dot_nt(qd[:, ks], kd[:, ks]), 0.0).astype(BF16)
            s_old = s_ref[h]
            o = _dot(att, v[:, vs]) + _dot(qd[:, ks], s_old.astype(BF16))
            dcol = jnp.sum(eye * dl[:, ks], axis=1, keepdims=True)
            s_ref[h] = s_old * dcol + _dot_tn(ko[:, ks], v[:, vs])
            ms = jnp.mean(o * o, axis=-1, keepdims=True)
            outs.append(o * lax.rsqrt(ms + EPS) * gout[:, vs])
        o_ref[rows, :] = (jnp.concatenate(outs, axis=1) * sg_ref[rows, :].astype(F32)).astype(BF16)
        return 0

    lax.fori_loop(0, n_chunks, chunk, 0)

    @pl.when(t == pl.num_programs(1) - 1)
    def _():
        sfin_ref[0] = s_ref[...]


def _gla(gq, gk, gv, la, sg, gout, s0, b, t, blk):
    row = lambda wd: pl.BlockSpec((blk, wd), lambda i, j: (i * (t // blk) + j, 0))
    st = pl.BlockSpec((1, GLA_HEADS, GLA_DK, GLA_DV), lambda i, j: (i, 0, 0, 0))
    ltri = jnp.asarray(np.tril(np.ones((CHUNK, CHUNK), np.float32)), BF16)
    return pl.pallas_call(
        _gla_kernel,
        grid=(b, t // blk),
        in_specs=[row(256), row(256), row(512), row(256), row(512), _full((1, 512)), st, _full((CHUNK, CHUNK))],
        out_specs=[row(512), st],
        out_shape=[jax.ShapeDtypeStruct((b * t, BRANCH_W), BF16),
                   jax.ShapeDtypeStruct((b, GLA_HEADS, GLA_DK, GLA_DV), F32)],
        scratch_shapes=[pltpu.VMEM((GLA_HEADS, GLA_DK, GLA_DV), F32)],
        compiler_params=_params("parallel", "arbitrary"),
        name="gla",
    )(gq, gk, gv, la, sg, gout, s0, ltri)


def _memkv_kernel(x_ref, g_ref, w_ref, gk_ref, bd128_ref, k_ref, v_ref):
    x = x_ref[...]
    ms = jnp.mean(x * x, axis=-1, keepdims=True)
    h = (x * lax.rsqrt(ms + EPS) * g_ref[...]).astype(BF16)
    kv = _dot(h, w_ref[...])
    k_ref[...] = _head_norm(kv[:, :512], bd128_ref, gk_ref[...])
    v_ref[...] = kv[:, 512:]


def _memkv(mem, g_in, w_kv, gk, bd128, tm):
    n = mem.shape[0]
    row = lambda wd: pl.BlockSpec((tm, wd), lambda i: (i, 0))
    return pl.pallas_call(
        _memkv_kernel,
        grid=(n // tm,),
        in_specs=[row(D_MODEL), _full((1, D_MODEL)), _full(w_kv.shape), _full((1, 512)), _full((512, 512))],
        out_specs=[row(512), row(512)],
        out_shape=[jax.ShapeDtypeStruct((n, 512), F32)] * 2,
        compiler_params=_params("parallel"),
        name="memkv",
    )(mem, g_in, w_kv, gk, bd128)


def _merge_kernel(x_ref, og_ref, of_ref, mq_ref, mk_ref, mv_ref, gate_ref, wbr_ref, wout_ref, gffn_ref,
                  x2_ref, h2_ref):
    mq = mq_ref[...]
    mk = mk_ref[0].astype(BF16)
    mv = mv_ref[0].astype(BF16)
    om = []
    for h in range(MEM_HEADS):
        hs = slice(h * MEM_DH, (h + 1) * MEM_DH)
        s = _dot_nt(mq[:, hs], mk[:, hs])
        e = jnp.exp(s - jnp.max(s, axis=1, keepdims=True))
        om.append(_dot(e.astype(BF16), mv[:, hs]) * (1.0 / jnp.sum(e, axis=1, keepdims=True)))
    branches = (og_ref[...], of_ref[...], jnp.concatenate(om, axis=1).astype(BF16))
    mix = None
    for b in range(N_BRANCH):
        term = gate_ref[:, b * D_MODEL:(b + 1) * D_MODEL].astype(F32) * _dot(branches[b], wbr_ref[b])
        mix = term if mix is None else mix + term
    x2 = x_ref[...] + _dot(mix.astype(BF16), wout_ref[...])
    x2_ref[...] = x2
    ms = jnp.mean(x2 * x2, axis=-1, keepdims=True)
    h2_ref[...] = (x2 * lax.rsqrt(ms + EPS) * gffn_ref[...]).astype(BF16)


def _merge(x, og, of, mq, mk, mv, gate, wbr, wout, gffn, t, tm):
    n = x.shape[0]
    row = lambda wd: pl.BlockSpec((tm, wd), lambda i: (i, 0))
    mem = pl.BlockSpec((1, N_MEM, 512), lambda i: (i // (t // tm), 0, 0))
    return pl.pallas_call(
        _merge_kernel,
        grid=(n // tm,),
        in_specs=[row(D_MODEL), row(512), row(512), row(512), mem, mem, row(3 * D_MODEL),
                  _full(wbr.shape), _full(wout.shape), _full((1, D_MODEL))],
        out_specs=[row(D_MODEL), row(D_MODEL)],
        out_shape=[jax.ShapeDtypeStruct((n, D_MODEL), F32), jax.ShapeDtypeStruct((n, D_MODEL), BF16)],
        compiler_params=_params("parallel"),
        name="merge",
    )(x, og, of, mq, mk, mv, gate, wbr, wout, gffn)


def _extract_top(s, pos, k, exact_ties):
    cur, vals = s, []
    rank = jnp.full(s.shape, -1.0, F32)
    for r in range(k):
        m = jnp.max(cur, axis=0, keepdims=True)
        hit = cur == m
        if exact_ties:
            hit = pos == jnp.min(jnp.where(hit, pos, 1e9), axis=0, keepdims=True)
        cur = jnp.where(hit, -jnp.inf, cur)
        rank = jnp.where(hit, float(r), rank)
        vals.append(m)
    return vals, rank


_CAND_COUNT = [PEER_TOPK // (a + 1) for a in range(PEER_TOPK)]


def _cand_positions():
    pos = np.full((80, 128), -1.0, np.float32)
    pos[0:16] = np.arange(16)[:, None]
    for a in range(1, 8):
        for b in range(_CAND_COUNT[a]):
            pos[8 * (a + 1) + b] = a * 16 + b
    pos[72:80] = (np.arange(8, 16) * 16)[:, None]
    return pos


def _select_group(s1, s2, rowf, cpos, exact_ties):
    v1, rank1 = _extract_top(s1, rowf, PEER_TOPK, exact_ties)
    v2, rank2 = _extract_top(s2, rowf, PEER_TOPK, exact_ties)
    v1_all = jnp.concatenate(v1, axis=0)
    v2_all = jnp.concatenate(v2, axis=0)
    pieces = [v2_all[0:8] + v1[0], v2_all[8:16] + v1[0]] + [v2_all[0:8] + v1[a] for a in range(1, 8)]
    pieces.append(v1_all[8:16] + v2[0])
    cand = jnp.where(cpos >= 0.0, jnp.concatenate(pieces, axis=0), -jnp.inf)
    _, taken = _extract_top(cand, cpos, PEER_TOPK, exact_ties)
    sel = jnp.where(taken >= 0.0, 1.0, 0.0)
    e = sel * jnp.exp(cand - (v1[0] + v2[0]))
    inv_z = 1.0 / jnp.sum(e, axis=0, keepdims=True)
    counts = [jnp.sum(sel[0:16], axis=0, keepdims=True)]
    counts += [jnp.sum(sel[8 * (a + 1):8 * (a + 2)], axis=0, keepdims=True) for a in range(1, 8)]
    counts += [sel[72 + a:73 + a] for a in range(8)]
    n1 = jnp.zeros(s1.shape, F32)
    for a in range(PEER_TOPK):
        n1 = jnp.where(rank1 == float(a), counts[a], n1)
    n_taken = (jnp.sum(jnp.where(rank1 >= 0.0, 1.0, 0.0), axis=0, keepdims=True)
               + jnp.sum(jnp.where(rank2 >= 0.0, 1.0, 0.0), axis=0, keepdims=True)
               + jnp.sum(sel, axis=0, keepdims=True))
    rk = jnp.where(rank2 >= 0.0, rank2, float(N_KEYS - 1))
    return jnp.exp(s1 - v1[0]) * inv_z, jnp.exp(s2 - v2[0]), n1, rk, n_taken


def _peer_select_kernel(h_ref, wqt_ref, sk_ref, cpos_ref, e1_ref, e2_ref, n1_ref, rk_ref):
    hb = h_ref[...]
    tm = hb.shape[0]
    rowf = lax.broadcasted_iota(jnp.int32, (N_KEYS, 128), 0).astype(F32)
    cpos = cpos_ref[...]

    def head(h, _):
        def scores(p):
            r = h * 2 + p
            q_t = _dot_nt(wqt_ref[pl.ds(pl.multiple_of(r * PEER_DHALF, PEER_DHALF), PEER_DHALF), :], hb)
            return _dot(sk_ref[r], q_t.astype(BF16))

        s1, s2 = scores(0), scores(1)
        for g in range(tm // 128):
            lanes = slice(g * 128, (g + 1) * 128)

            def tables(exact_ties):
                e1, e2, n1, rk, n_taken = _select_group(s1[:, lanes], s2[:, lanes], rowf, cpos, exact_ties)
                e1_ref[h, :, lanes] = e1
                e2_ref[h, :, lanes] = e2.astype(BF16)
                n1_ref[h, :, lanes] = n1
                rk_ref[h, :, lanes] = rk.astype(BF16)
                return n_taken

            n_taken = tables(False)

            @pl.when(jnp.max(n_taken) > 3.0 * PEER_TOPK)
            def _():
                tables(True)
        return 0

    lax.fori_loop(0, PEER_HEADS, head, 0)


def _peer_select(h2, wqt, sk, tm):
    n = h2.shape[0]
    tab = pl.BlockSpec((PEER_HEADS, N_KEYS, tm), lambda i: (0, 0, i))
    shp = lambda dt: jax.ShapeDtypeStruct((PEER_HEADS, N_KEYS, n), dt)
    cpos = jnp.asarray(_cand_positions())
    return pl.pallas_call(
        _peer_select_kernel,
        grid=(n // tm,),
        in_specs=[pl.BlockSpec((tm, D_MODEL), lambda i: (i, 0)), _full(wqt.shape), _full(sk.shape), _full(cpos.shape)],
        out_specs=[tab] * 4,
        out_shape=[shp(F32), shp(BF16), shp(F32), shp(BF16)],
        compiler_params=_params("parallel"),
        name="peer_select",
    )(h2, wqt, sk, cpos)


def _peer_ffn_kernel(x2_ref, h_ref, e1_ref, e2_ref, n1_ref, rk_ref, u_ref, v_ref, y_ref, wt_ref, acc_ref):
    e = pl.program_id(1)

    @pl.when(e == 0)
    def _():
        acc_ref[...] = jnp.zeros_like(acc_ref)

    eb, tm = wt_ref.shape
    a_t = _dot_nt(u_ref[...], h_ref[...])
    zero = jnp.zeros((N_KEYS, tm), BF16)
    for ii in range(eb // N_KEYS):
        a = a_t[ii * N_KEYS:(ii + 1) * N_KEYS, :]
        gate = None
        for h in range(PEER_HEADS):
            n1 = jnp.broadcast_to(n1_ref[h, ii:ii + 1, :], (N_KEYS, tm)).astype(BF16)
            e1 = jnp.broadcast_to(e1_ref[h, ii:ii + 1, :], (N_KEYS, tm)).astype(BF16)
            term = jnp.where(rk_ref[h] < n1, e2_ref[h], zero) * e1
            gate = term if gate is None else gate + term
        gelu = 0.5 * a * (1.0 + lax.erf(a * (2.0 ** -0.5)))
        wt_ref[ii * N_KEYS:(ii + 1) * N_KEYS, :] = gate * gelu.astype(BF16)
    acc_ref[...] += _dot_tn(wt_ref[...], v_ref[...])

    @pl.when(e == pl.num_programs(1) - 1)
    def _():
        y_ref[...] = x2_ref[...] + acc_ref[...]


def _peer_ffn(x2, h2, e1, e2, n1, rk, u, v, tm, eb):
    n = x2.shape[0]
    row = lambda: pl.BlockSpec((tm, D_MODEL), lambda i, j: (i, 0))
    tab = pl.BlockSpec((PEER_HEADS, N_KEYS, tm), lambda i, j: (0, 0, i))
    first = pl.BlockSpec((PEER_HEADS, eb // N_KEYS, tm), lambda i, j: (0, j, i))
    ex = pl.BlockSpec((eb, D_MODEL), lambda i, j: (j, 0))
    return pl.pallas_call(
        _peer_ffn_kernel,
        grid=(n // tm, N_EXPERTS // eb),
        in_specs=[row(), row(), first, tab, first, tab, ex, ex],
        out_specs=row(),
        out_shape=jax.ShapeDtypeStruct((n, D_MODEL), F32),
        scratch_shapes=[pltpu.VMEM((eb, tm), BF16), pltpu.VMEM((tm, D_MODEL), F32)],
        compiler_params=_params("parallel", "arbitrary"),
        name="peer_ffn",
    )(x2, h2, e1, e2, n1, rk, u, v)


def _block_diag_mean(width, head):
    i = np.arange(width)
    return jnp.asarray((i[:, None] // head == i[None, :] // head) / head, BF16)


def _tile(n, pref):
    return pref if n % pref == 0 else n


def _cumsum_tile(t):
    return max(d for d in range(16, min(t, 1024) + 1, 16) if t % d == 0)


def _prepare_weights(l, g_attn, w_in, gla_w_a2, gla_b_a, gla_g_out, fox_b_f, fox_g_q, fox_g_k, mem_g_q, mem_g_in,
                     mem_w_kv, mem_g_k, w_branch, w_out, g_ffn, peer_w_q, peer_sub_keys, peer_u, peer_v):
    w = w_in[l]
    wr = jnp.concatenate([w[:, 0:1536], w[:, 1552:3088], w[:, 3096:6680], w[:, 3088:3096], w[:, 1536:1552],
                          jnp.zeros((D_MODEL, W_COLS - 6680), F32)], axis=1).astype(BF16)
    wa2p = jnp.zeros((128, GLA_HEADS * GLA_DK), F32).at[SM_GA:SM_GA + GLA_RANK].set(gla_w_a2[l]).astype(BF16)
    bsm = jnp.zeros((1, 128), F32).at[0, SM_FF:SM_FF + FOX_HEADS].set(fox_b_f[l])
    return dict(
        g_attn=g_attn[l][None], w=wr, wa2p=wa2p, bsm=bsm, ba=gla_b_a[l][None],
        gfq=jnp.tile(fox_g_q[l], FOX_HEADS)[None], gfk=jnp.tile(fox_g_k[l], FOX_HEADS)[None],
        gmq=jnp.tile(mem_g_q[l], MEM_HEADS)[None], gout=jnp.tile(gla_g_out[l], GLA_HEADS)[None],
        bd64=_block_diag_mean(512, FOX_DH), bd128=_block_diag_mean(512, MEM_DH),
        mem_g_in=mem_g_in[l][None], mem_w_kv=mem_w_kv[l].astype(BF16), mem_g_k=jnp.tile(mem_g_k[l], MEM_HEADS)[None],
        wbr=w_branch[l].astype(BF16), wout=w_out[l].astype(BF16), gffn=g_ffn[l][None],
        wqt=peer_w_q[l].T.astype(BF16),
        sk=peer_sub_keys[l].reshape(PEER_HEADS * 2, N_KEYS, PEER_DHALF).astype(BF16),
        u=peer_u[l].astype(BF16), v=peer_v[l].astype(BF16))


def _trunk(x, mem_k, mem_v, fox_past, s0, p):
    b, t, _ = x.shape
    n = b * t
    xf = x.reshape(n, D_MODEL)
    gq, gk, gv, la, sg, fq, fk, fv, mq, gate, lsm = _inproj(
        xf, p["g_attn"], p["w"], p["wa2p"], p["bsm"], p["ba"], p["gfq"], p["gfk"], p["gmq"], p["bd64"], p["bd128"],
        _tile(n, 256))
    o_gla, s_new = _gla(gq, gk, gv, la, sg, p["gout"], s0, b, t, _tile(t, 512))
    if fox_past is None:
        c, _ = _cumsum(lsm.reshape(b, t, 128), _cumsum_tile(t))
        qh, kh, vh = _foxprep(fq, fk, fv, c.reshape(n, 128), b, t, _tile(t, 256))
        o_fox = _fox_prompt(qh, kh, vh, _tile(t, 512)).reshape(n, BRANCH_W)
    else:
        k_past, v_past, lf_past = fox_past
        past = k_past.shape[1]
        lf = jnp.concatenate([jnp.pad(lf_past, ((0, 0), (0, 0), (0, 128 - FOX_HEADS))), lsm.reshape(b, t, 128)], axis=1)
        c_all, caug = _cumsum(lf, _cumsum_tile(past + t))
        o_fox = _fox_sample(fq.reshape(b, t, 512), c_all, caug, k_past, v_past, fk.reshape(b, t, 512),
                            fv.reshape(b, t, 512), _tile(past, 512)).reshape(n, BRANCH_W)
    x2, h2 = _merge(xf, o_gla, o_fox, mq, mem_k, mem_v, gate, p["wbr"], p["wout"], p["gffn"], t, _tile(t, 256))
    e1, e2, n1, rk = _peer_select(h2, p["wqt"], p["sk"], _tile(n, 256))
    y = _peer_ffn(x2, h2, e1, e2, n1, rk, p["u"], p["v"], _tile(n, 512), 1024)
    logf = lsm[:, SM_FF:SM_FF + FOX_HEADS]
    return (y.reshape(b, t, D_MODEL), fk.reshape(b, t, FOX_HEADS, FOX_DH), fv.reshape(b, t, FOX_HEADS, FOX_DH),
            logf.reshape(b, t, FOX_HEADS), s_new)


def kernel(x_prompt, x_sample, mem_prompt, cache_fox_k, cache_fox_v, cache_fox_logf, state_gla, cache_mem_k, cache_mem_v, g_attn, w_in, gla_w_a2, gla_b_a, gla_g_out, fox_b_f, fox_g_q, fox_g_k, mem_g_q, mem_g_in, mem_w_kv, mem_g_k, w_branch, w_out, g_ffn, peer_w_q, peer_sub_keys, peer_u, peer_v):
    depth = w_in.shape[0]
    yp, ys = x_prompt, x_sample
    bp, bs = x_prompt.shape[0], x_sample.shape[0]
    outs = [[] for _ in range(10)]
    for l in range(depth):
        p = _prepare_weights(l, g_attn, w_in, gla_w_a2, gla_b_a, gla_g_out, fox_b_f, fox_g_q, fox_g_k, mem_g_q,
                             mem_g_in, mem_w_kv, mem_g_k, w_branch, w_out, g_ffn, peer_w_q, peer_sub_keys, peer_u,
                             peer_v)
        mk, mv = _memkv(mem_prompt.reshape(bp * N_MEM, D_MODEL), p["mem_g_in"], p["mem_w_kv"], p["mem_g_k"],
                        p["bd128"], _tile(bp * N_MEM, 256))
        mk = mk.reshape(bp, N_MEM, 512)
        mv = mv.reshape(bp, N_MEM, 512)
        s0 = jnp.zeros((bp, GLA_HEADS, GLA_DK, GLA_DV), F32)
        yp, k1, v1, lf1, s1 = _trunk(yp, mk, mv, None, s0, p)
        past = cache_fox_k.shape[2]
        ys, k2, v2, lf2, s2 = _trunk(
            ys, cache_mem_k[l].reshape(bs, N_MEM, 512), cache_mem_v[l].reshape(bs, N_MEM, 512),
            (cache_fox_k[l].reshape(bs, past, 512), cache_fox_v[l].reshape(bs, past, 512), cache_fox_logf[l]),
            state_gla[l], p)
        for lst, val in zip(outs, (k1, v1, lf1, s1, mk.reshape(bp, N_MEM, MEM_HEADS, MEM_DH),
                                   mv.reshape(bp, N_MEM, MEM_HEADS, MEM_DH), k2, v2, lf2, s2)):
            lst.append(val)
    return (yp, ys) + tuple(jnp.stack(o) for o in outs)
```
